```python
import math
import jax, jax.numpy as jnp
from jax import lax
import numpy as np

D_MODEL = 1024
BATCH = 4
SEQ = 4096
DEPTH = 2
DEC_BATCH = 8
DEC_SEQ = 8192
PAST_LEN = 128

N_MEM = 256
N_MIXERS = 2
N_MLA_LAYERS = (DEPTH + 1) // 2
N_DIFF_LAYERS = DEPTH // 2

MLA_HEADS = 8
MLA_NOPE = 128
MLA_ROPE = 64
MLA_V = 128
Q_LORA = 256
KV_LORA = 256
ROPE_THETA = 10000.0

DIFF_HEADS = 8
DIFF_HD = D_MODEL // DIFF_HEADS // 2
DIFF_VD = 2 * DIFF_HD

XATTN_HEADS = 4
XATTN_HD = D_MODEL // XATTN_HEADS

D_FF = 4 * D_MODEL
N_BUCKETS = 32
MAX_DISTANCE = 128
Q_BLOCK = 128
EPS = 1e-6

kernel_name = 'hybrid_mla_diffattn_encoder'


def rms_norm(x, g):
    xf = x.astype(jnp.float32)
    y = xf * lax.rsqrt(jnp.mean(xf * xf, axis=-1, keepdims=True) + EPS)
    return (y * g.astype(jnp.float32)).astype(x.dtype)


def rope(x, pos):
    half = x.shape[-1] // 2
    freqs = ROPE_THETA ** (-jnp.arange(half, dtype=jnp.float32) / half)
    ang = pos.astype(jnp.float32)[:, None] * freqs[None, :]
    cos = jnp.cos(ang)[:, None, :]
    sin = jnp.sin(ang)[:, None, :]
    xf = x.astype(jnp.float32)
    x1, x2 = xf[..., :half], xf[..., half:]
    return jnp.concatenate([x1 * cos - x2 * sin, x1 * sin + x2 * cos], axis=-1).astype(x.dtype)


def t5_bucket(rel):
    nb = N_BUCKETS // 2
    max_exact = nb // 2
    ret = (rel > 0).astype(jnp.int32) * nb
    n = jnp.abs(rel)
    large = max_exact + (jnp.log(jnp.maximum(n, 1).astype(jnp.float32) / max_exact)
                         / math.log(MAX_DISTANCE / max_exact) * (nb - max_exact)).astype(jnp.int32)
    large = jnp.minimum(large, nb - 1)
    return ret + jnp.where(n < max_exact, n, large)


def sweep_query_blocks(block_fn, *qs):
    b, s = qs[0].shape[:2]
    nb = s // Q_BLOCK
    blocks = tuple(jnp.moveaxis(q.reshape(b, nb, Q_BLOCK, *q.shape[2:]), 1, 0) for q in qs)
    starts = jnp.arange(nb, dtype=jnp.int32) * Q_BLOCK
    out = lax.map(lambda a: block_fn(*a), (starts,) + blocks)
    out = jnp.moveaxis(out, 0, 1)
    return out.reshape(b, s, *out.shape[3:])


def mla_mixer(h, w_in, q_norm, kv_norm, w_uq, w_ukv, w_o):
    b, s, _ = h.shape
    proj = h @ w_in
    c_q, c_kv, k_r = jnp.split(proj, [Q_LORA, Q_LORA + KV_LORA], axis=-1)
    q = (rms_norm(c_q, q_norm) @ w_uq).reshape(b, s, MLA_HEADS, MLA_NOPE + MLA_ROPE)
    kv = (rms_norm(c_kv, kv_norm) @ w_ukv).reshape(b, s, MLA_HEADS, MLA_NOPE + MLA_V)
    q_nope, q_rope = q[..., :MLA_NOPE], q[..., MLA_NOPE:]
    k_nope, v = kv[..., :MLA_NOPE], kv[..., MLA_NOPE:]
    pos = jnp.arange(s, dtype=jnp.int32)
    q_rope = rope(q_rope, pos)
    k_rope = rope(k_r[:, :, None, :], pos)[:, :, 0, :]
    scale = (MLA_NOPE + MLA_ROPE) ** -0.5

    def block(start, qn, qr):
        sc = (jnp.einsum('bqhd,bkhd->bhqk', qn, k_nope)
              + jnp.einsum('bqhr,bkr->bhqk', qr, k_rope)).astype(jnp.float32) * scale
        p = jax.nn.softmax(sc, axis=-1).astype(v.dtype)
        return jnp.einsum('bhqk,bkhd->bqhd', p, v)

    o = sweep_query_blocks(block, q_nope, q_rope)
    return o.reshape(b, s, MLA_HEADS * MLA_V) @ w_o


def diff_mixer(h, layer_idx, rel_bias_table, w_in, lam, subln, w_o):
    b, s, _ = h.shape
    q, k, v = jnp.split(h @ w_in, 3, axis=-1)
    q = q.reshape(b, s, DIFF_HEADS, 2, DIFF_HD)
    k = k.reshape(b, s, DIFF_HEADS, 2, DIFF_HD)
    v = v.reshape(b, s, DIFF_HEADS, DIFF_VD)
    lambda_init = 0.8 - 0.6 * math.exp(-0.3 * layer_idx)
    lf = lam.astype(jnp.float32)
    lam_full = jnp.exp(jnp.sum(lf[0] * lf[1])) - jnp.exp(jnp.sum(lf[2] * lf[3])) + lambda_init
    scale = DIFF_HD ** -0.5
    kpos = jnp.arange(s, dtype=jnp.int32)

    def block(start, qb):
        sc = jnp.einsum('bqhcd,bkhcd->bchqk', qb, k).astype(jnp.float32) * scale
        rel = kpos[None, :] - (start + jnp.arange(Q_BLOCK, dtype=jnp.int32))[:, None]
        bias = jnp.moveaxis(rel_bias_table[t5_bucket(rel)], -1, 0).astype(jnp.float32)
        p = jax.nn.softmax(sc + bias, axis=-1)
        a = (p[:, 0] - lam_full * p[:, 1]).astype(v.dtype)
        return jnp.einsum('bhqk,bkhd->bqhd', a, v)

    o = sweep_query_blocks(block, q)
    o = (rms_norm(o, subln) * (1.0 - lambda_init)).astype(h.dtype)
    return o.reshape(b, s, DIFF_HEADS * DIFF_VD) @ w_o


def memory_cross_attention(h, mem, mem_norm, w_q, w_kv, w_o):
    b, s, _ = h.shape
    m = rms_norm(mem, mem_norm)
    q = (h @ w_q).reshape(b, s, XATTN_HEADS, XATTN_HD)
    k, v = jnp.split(m @ w_kv, 2, axis=-1)
    k = k.reshape(b, N_MEM, XATTN_HEADS, XATTN_HD)
    v = v.reshape(b, N_MEM, XATTN_HEADS, XATTN_HD)
    sc = jnp.einsum('bqhd,bkhd->bhqk', q, k).astype(jnp.float32) * (XATTN_HD ** -0.5)
    p = jax.nn.softmax(sc, axis=-1).astype(v.dtype)
    o = jnp.einsum('bhqk,bkhd->bqhd', p, v).reshape(b, s, D_MODEL)
    return o @ w_o


def squared_relu_mlp(h, w1, w2):
    return jnp.square(jax.nn.relu(h @ w1)) @ w2


def trunk(x, mem, norm_gains, rel_bias_table,
          mla_w_in, mla_q_norm, mla_kv_norm, mla_w_uq, mla_w_ukv, mla_w_o,
          diff_w_in, diff_lambda, diff_subln, diff_w_o,
          xattn_mem_norm, xattn_w_q, xattn_w_kv, xattn_w_o,
          mlp_w1, mlp_w2):
    for i in range(DEPTH):
        g = norm_gains[i]
        j = i // N_MIXERS
        h = rms_norm(x, g[0])
        if i % N_MIXERS == 0:
            h = mla_mixer(h, mla_w_in[j], mla_q_norm[j], mla_kv_norm[j],
                          mla_w_uq[j], mla_w_ukv[j], mla_w_o[j])
        else:
            h = diff_mixer(h, i, rel_bias_table, diff_w_in[j], diff_lambda[j],
                           diff_subln[j], diff_w_o[j])
        x = x + rms_norm(h, g[1])
        h = memory_cross_attention(rms_norm(x, g[2]), mem, xattn_mem_norm[i],
                                   xattn_w_q[i], xattn_w_kv[i], xattn_w_o[i])
        x = x + rms_norm(h, g[3])
        h = squared_relu_mlp(rms_norm(x, g[4]), mlp_w1[i], mlp_w2[i])
        x = x + rms_norm(h, g[5])
    return x


def setup_inputs(seed: int = 0) -> dict:
    key = jax.random.key(seed)
    ks = jax.random.split(key, 24)
    f32 = jnp.float32

    def w(k, shape, fan_in):
        return jax.random.normal(k, shape, f32) * (fan_in ** -0.5)

    def gain(k, shape):
        return 1.0 + 0.01 * jax.random.normal(k, shape, f32)

    na, nd = N_MLA_LAYERS, N_DIFF_LAYERS
    return {
        'x_prompt': jax.random.normal(ks[0], (BATCH, SEQ, D_MODEL), f32),
        'x_sample': jax.random.normal(ks[1], (DEC_BATCH, DEC_SEQ, D_MODEL), f32),
        'mem_prompt': jax.random.normal(ks[2], (BATCH, N_MEM, D_MODEL), f32),
        'mem_sample': jax.random.normal(ks[3], (DEC_BATCH, N_MEM, D_MODEL), f32),
        'norm_gains': gain(ks[4], (DEPTH, 6, D_MODEL)),
        'rel_bias_table': 0.1 * jax.random.normal(ks[5], (N_BUCKETS, DIFF_HEADS), f32),
        'mla_w_in': w(ks[6], (na, D_MODEL, Q_LORA + KV_LORA + MLA_ROPE), D_MODEL),
        'mla_q_norm': gain(ks[7], (na, Q_LORA)),
        'mla_kv_norm': gain(ks[8], (na, KV_LORA)),
        'mla_w_uq': w(ks[9], (na, Q_LORA, MLA_HEADS * (MLA_NOPE + MLA_ROPE)), Q_LORA),
        'mla_w_ukv': w(ks[10], (na, KV_LORA, MLA_HEADS * (MLA_NOPE + MLA_V)), KV_LORA),
        'mla_w_o': w(ks[11], (na, MLA_HEADS * MLA_V, D_MODEL), MLA_HEADS * MLA_V),
        'diff_w_in': w(ks[12], (nd, D_MODEL, 3 * D_MODEL), D_MODEL),
        'diff_lambda': 0.1 * jax.random.normal(ks[13], (nd, 4, DIFF_HD), f32),
        'diff_subln': gain(ks[14], (nd, DIFF_VD)),
        'diff_w_o': w(ks[15], (nd, DIFF_HEADS * DIFF_VD, D_MODEL), DIFF_HEADS * DIFF_VD),
        'xattn_mem_norm': gain(ks[16], (DEPTH, D_MODEL)),
        'xattn_w_q': w(ks[17], (DEPTH, D_MODEL, D_MODEL), D_MODEL),
        'xattn_w_kv': w(ks[18], (DEPTH, D_MODEL, 2 * D_MODEL), D_MODEL),
        'xattn_w_o': w(ks[19], (DEPTH, D_MODEL, D_MODEL), D_MODEL),
        'mlp_w1': w(ks[20], (DEPTH, D_MODEL, D_FF), D_MODEL),
        'mlp_w2': w(ks[21], (DEPTH, D_FF, D_MODEL), D_FF),
    }


def reference(x_prompt, x_sample, mem_prompt, mem_sample, norm_gains, rel_bias_table,
              mla_w_in, mla_q_norm, mla_kv_norm, mla_w_uq, mla_w_ukv, mla_w_o,
              diff_w_in, diff_lambda, diff_subln, diff_w_o,
              xattn_mem_norm, xattn_w_q, xattn_w_kv, xattn_w_o,
              mlp_w1, mlp_w2):
    y_prompt = trunk(x_prompt, mem_prompt, norm_gains, rel_bias_table,
                     mla_w_in, mla_q_norm, mla_kv_norm, mla_w_uq, mla_w_ukv, mla_w_o,
                     diff_w_in, diff_lambda, diff_subln, diff_w_o,
                     xattn_mem_norm, xattn_w_q, xattn_w_kv, xattn_w_o,
                     mlp_w1, mlp_w2)
    y_sample = trunk(x_sample, mem_sample, norm_gains, rel_bias_table,
                     mla_w_in, mla_q_norm, mla_kv_norm, mla_w_uq, mla_w_ukv, mla_w_o,
                     diff_w_in, diff_lambda, diff_subln, diff_w_o,
                     xattn_mem_norm, xattn_w_q, xattn_w_kv, xattn_w_o,
                     mlp_w1, mlp_w2)
    return (y_prompt, y_sample)
```

```python
import functools
import math

import jax
import jax.numpy as jnp
from jax import lax
from jax.experimental import pallas as pl
from jax.experimental.pallas import tpu as pltpu

EPS = 1e-6
ROPE_THETA = 10000.0
LOG2E = math.log2(math.e)

MLA_HEADS = 8
MLA_NOPE = 128
MLA_ROPE = 64
MLA_V = 128
MLA_QK_PAD = 256
Q_LORA = 256
KV_LORA = 256

DIFF_HEADS = 8
DIFF_HD = 64
DIFF_VD = 128

XATTN_HEADS = 4
N_BUCKETS = 32
MAX_DISTANCE = 128
N_BIAS_TILES = 5

LANES = 128
ROW_TILE = 512
ATTN_TILE = 512
VMEM_LIMIT = 56 * 1024 * 1024

BF16 = jnp.bfloat16
F32 = jnp.float32

_NT = (((1,), (1,)), ((), ()))


def _rms(x, g):
    ms = jnp.mean(x * x, axis=-1, keepdims=True)
    return x * lax.rsqrt(ms + EPS) * g


def _dot(a, b):
    return jnp.dot(a, b, preferred_element_type=F32)


def _dot_nt(a, b):
    return lax.dot_general(a, b, _NT, preferred_element_type=F32)


def _params(*sem):
    return pltpu.CompilerParams(dimension_semantics=sem, vmem_limit_bytes=VMEM_LIMIT)


def _const_spec(shape):
    nd = len(shape)
    return pl.BlockSpec(shape, lambda *_: (0,) * nd)


def _mla_proj_kernel(x_ref, g_ref, w_in_ref, qg_ref, kvg_ref, w_uq_ref, w_k_ref, w_vt_ref,
                     tabq_ref, tabk_ref, q_ref, kn_ref, kr_ref, vt_ref, *, q_scale):
    h = _rms(x_ref[0], g_ref[...]).astype(BF16)
    proj = _dot(h, w_in_ref[...])
    cq = _rms(proj[:, :Q_LORA], qg_ref[...]).astype(BF16)
    ckv = _rms(proj[:, Q_LORA:Q_LORA + KV_LORA], kvg_ref[...]).astype(BF16)
    t = proj[:, Q_LORA + KV_LORA:] * tabk_ref[...]
    kr_ref[0] = (t + pltpu.roll(t, LANES // 2, axis=1)).astype(BF16)
    q = _dot(cq, w_uq_ref[...])
    tabq = tabq_ref[...]
    for hd in range(MLA_HEADS):
        base = hd * MLA_QK_PAD
        q_ref[0, :, base:base + MLA_NOPE] = (q[:, base:base + MLA_NOPE] * q_scale).astype(BF16)
        q_ref[0, :, base + MLA_NOPE:base + MLA_QK_PAD] = (
            q[:, base + MLA_NOPE:base + MLA_QK_PAD] * tabq).astype(BF16)
    kn_ref[0] = _dot(ckv, w_k_ref[...]).astype(BF16)
    vt_ref[0] = _dot_nt(w_vt_ref[...], ckv).astype(BF16)


def _mla_proj(x, g, w_in_aug, qg, kvg, w_uq_aug, w_k, w_vt, tabq, tabk, q_scale):
    b, s, d = x.shape
    tm = min(ROW_TILE, s)
    hq = MLA_HEADS * MLA_QK_PAD
    hk = MLA_HEADS * MLA_NOPE
    hv = MLA_HEADS * MLA_V
    return pl.pallas_call(
        functools.partial(_mla_proj_kernel, q_scale=q_scale),
        grid=(b, s // tm),
        in_specs=[
            pl.BlockSpec((1, tm, d), lambda i, j: (i, j, 0)),
            _const_spec(g.shape), _const_spec(w_in_aug.shape), _const_spec(qg.shape),
            _const_spec(kvg.shape), _const_spec(w_uq_aug.shape), _const_spec(w_k.shape),
            _const_spec(w_vt.shape),
            pl.BlockSpec((tm, LANES), lambda i, j: (j, 0)),
            pl.BlockSpec((tm, LANES), lambda i, j: (j, 0)),
        ],
        out_specs=[
            pl.BlockSpec((1, tm, hq), lambda i, j: (i, j, 0)),
            pl.BlockSpec((1, tm, hk), lambda i, j: (i, j, 0)),
            pl.BlockSpec((1, tm, LANES), lambda i, j: (i, j, 0)),
            pl.BlockSpec((1, hv, tm), lambda i, j: (i, 0, j)),
        ],
        out_shape=[
            jax.ShapeDtypeStruct((b, s, hq), BF16),
            jax.ShapeDtypeStruct((b, s, hk), BF16),
            jax.ShapeDtypeStruct((b, s, LANES), BF16),
            jax.ShapeDtypeStruct((b, hv, s), BF16),
        ],
        compiler_params=_params("parallel", "parallel"),
        name="mla_proj",
    )(x, g, w_in_aug, qg, kvg, w_uq_aug, w_k, w_vt, tabq, tabk)


def _mla_attn_kernel(q_ref, kn_ref, kr_ref, vt_ref, o_ref, m_sc, l_sc, acc_sc):
    ki = pl.program_id(3)

    @pl.when(ki == 0)
    def _():
        m_sc[...] = jnp.full(m_sc.shape, -jnp.inf, F32)
        l_sc[...] = jnp.zeros(l_sc.shape, F32)
        acc_sc[...] = jnp.zeros(acc_sc.shape, F32)

    k = jnp.concatenate([kn_ref[0], kr_ref[0]], axis=1)
    s = _dot_nt(k, q_ref[0])
    m_prev = m_sc[...]
    m_new = jnp.maximum(m_prev, jnp.max(s, axis=0, keepdims=True))
    alpha = jnp.exp2(m_prev - m_new)
    p = jnp.exp2(s - m_new)
    l_sc[...] = alpha * l_sc[...] + jnp.sum(p, axis=0, keepdims=True)
    acc_sc[...] = alpha * acc_sc[...] + _dot(vt_ref[0], p.astype(BF16))
    m_sc[...] = m_new

    @pl.when(ki == pl.num_programs(3) - 1)
    def _():
        o_ref[0] = (acc_sc[...] / l_sc[...]).T.astype(o_ref.dtype)


def _mla_attn(q, kn, kr, vt):
    b, s, _ = q.shape
    t = min(ATTN_TILE, s)
    return pl.pallas_call(
        _mla_attn_kernel,
        grid=(b, MLA_HEADS, s // t, s // t),
        in_specs=[
            pl.BlockSpec((1, t, MLA_QK_PAD), lambda bi, h, qi, ki: (bi, qi, h)),
            pl.BlockSpec((1, t, MLA_NOPE), lambda bi, h, qi, ki: (bi, ki, h)),
            pl.BlockSpec((1, t, LANES), lambda bi, h, qi, ki: (bi, ki, 0)),
            pl.BlockSpec((1, MLA_V, t), lambda bi, h, qi, ki: (bi, h, ki)),
        ],
        out_specs=pl.BlockSpec((1, t, MLA_V), lambda bi, h, qi, ki: (bi, qi, h)),
        out_shape=jax.ShapeDtypeStruct((b, s, MLA_HEADS * MLA_V), BF16),
        scratch_shapes=[
            pltpu.VMEM((1, t), F32), pltpu.VMEM((1, t), F32), pltpu.VMEM((MLA_V, t), F32),
        ],
        compiler_params=_params("parallel", "parallel", "parallel", "arbitrary"),
        name="mla_attn",
    )(q, kn, kr, vt)


def _diff_proj_kernel(x_ref, g_ref, w_q_ref, w_k_ref, w_vt_ref, q_ref, k_ref, vt_ref, *, q_scale):
    h = _rms(x_ref[0], g_ref[...]).astype(BF16)
    q = _dot(h, w_q_ref[...]) * q_scale
    first_half = (lax.broadcasted_iota(jnp.int32, q.shape, 1) % (2 * DIFF_HD)) < DIFF_HD
    q_ref[0, 0] = jnp.where(first_half, q, 0.0).astype(BF16)
    q_ref[0, 1] = jnp.where(first_half, 0.0, q).astype(BF16)
    k_ref[0] = _dot(h, w_k_ref[...]).astype(BF16)
    vt_ref[0] = _dot_nt(w_vt_ref[...], h).astype(BF16)


def _diff_proj(x, g, w_q, w_k, w_vt, q_scale):
    b, s, d = x.shape
    tm = min(ROW_TILE, s)
    n = DIFF_HEADS * DIFF_VD
    return pl.pallas_call(
        functools.partial(_diff_proj_kernel, q_scale=q_scale),
        grid=(b, s // tm),
        in_specs=[
            pl.BlockSpec((1, tm, d), lambda i, j: (i, j, 0)),
            _const_spec(g.shape), _const_spec(w_q.shape), _const_spec(w_k.shape),
            _const_spec(w_vt.shape),
        ],
        out_specs=[
            pl.BlockSpec((1, 2, tm, n), lambda i, j: (i, 0, j, 0)),
            pl.BlockSpec((1, tm, n), lambda i, j: (i, j, 0)),
            pl.BlockSpec((1, n, tm), lambda i, j: (i, 0, j)),
        ],
        out_shape=[
            jax.ShapeDtypeStruct((b, 2, s, n), BF16),
            jax.ShapeDtypeStruct((b, s, n), BF16),
            jax.ShapeDtypeStruct((b, n, s), BF16),
        ],
        compiler_params=_params("parallel", "parallel"),
        name="diff_proj",
    )(x, g, w_q, w_k, w_vt)


def _bias_tiles_kernel(table_ref, bucket_ref, o_ref):
    h = pl.program_id(0)
    bucket = bucket_ref[0]
    acc = jnp.zeros(bucket.shape, F32)
    for bkt in range(N_BUCKETS):
        acc = jnp.where(bucket == bkt, table_ref[bkt, h], acc)
    o_ref[0, 0] = acc * LOG2E


def _bias_tiles(table, bucket_tiles):
    n_tiles, t, _ = bucket_tiles.shape
    return pl.pallas_call(
        _bias_tiles_kernel,
        grid=(DIFF_HEADS, n_tiles),
        in_specs=[
            pl.BlockSpec(memory_space=pltpu.SMEM),
            pl.BlockSpec((1, t, t), lambda h, j: (j, 0, 0)),
        ],
        out_specs=pl.BlockSpec((1, 1, t, t), lambda h, j: (h, j, 0, 0)),
        out_shape=jax.ShapeDtypeStruct((DIFF_HEADS, n_tiles, t, t), F32),
        compiler_params=_params("parallel", "parallel"),
        name="t5_bias_tiles",
    )(table, bucket_tiles)


def _diff_attn_kernel(lam_ref, subln_ref, q_ref, k_ref, vt_ref, bias_ref, o_ref,
                      m_sc, l_sc, acc_sc, *, lambda_init):
    ki = pl.program_id(3)
    tq = o_ref.shape[1]

    @pl.when(ki == 0)
    def _():
        m_sc[...] = jnp.full(m_sc.shape, -jnp.inf, F32)
        l_sc[...] = jnp.zeros(l_sc.shape, F32)
        acc_sc[...] = jnp.zeros(acc_sc.shape, F32)

    q = q_ref[0].reshape(2 * tq, q_ref.shape[3])
    bias = bias_ref[0, 0]
    s = _dot_nt(k_ref[0], q) + jnp.concatenate([bias, bias], axis=1)
    m_prev = m_sc[...]
    m_new = jnp.maximum(m_prev, jnp.max(s, axis=0, keepdims=True))
    alpha = jnp.exp2(m_prev - m_new)
    p = jnp.exp2(s - m_new)
    l_sc[...] = alpha * l_sc[...] + jnp.sum(p, axis=0, keepdims=True)
    acc_sc[...] = alpha * acc_sc[...] + _dot(vt_ref[0], p.astype(BF16))
    m_sc[...] = m_new

    @pl.when(ki == pl.num_programs(3) - 1)
    def _():
        lam = lam_ref[...]
        lam_full = (jnp.exp(jnp.sum(lam[0:1] * lam[1:2], axis=1, keepdims=True))
                    - jnp.exp(jnp.sum(lam[2:3] * lam[3:4], axis=1, keepdims=True))
                    + lambda_init)
        o = acc_sc[...] / l_sc[...]
        o = o[:, :tq] - lam_full * o[:, tq:]
        ms = jnp.mean(o * o, axis=0, keepdims=True)
        o = o * lax.rsqrt(ms + EPS) * subln_ref[...] * (1.0 - lambda_init)
        o_ref[0] = o.T.astype(o_ref.dtype)


def _diff_attn(lam, subln_col, q, k, vt, bias, lambda_init):
    b, _, s, n = q.shape
    t = bias.shape[-1]
    lo, hi = 0, N_BIAS_TILES - 1
    half = N_BIAS_TILES // 2
    return pl.pallas_call(
        functools.partial(_diff_attn_kernel, lambda_init=lambda_init),
        grid=(b, DIFF_HEADS, s // t, s // t),
        in_specs=[
            _const_spec(lam.shape), _const_spec(subln_col.shape),
            pl.BlockSpec((1, 2, t, DIFF_VD), lambda bi, h, qi, ki: (bi, 0, qi, h)),
            pl.BlockSpec((1, t, DIFF_VD), lambda bi, h, qi, ki: (bi, ki, h)),
            pl.BlockSpec((1, DIFF_VD, t), lambda bi, h, qi, ki: (bi, h, ki)),
            pl.BlockSpec((1, 1, t, t),
                         lambda bi, h, qi, ki: (h, jnp.clip(ki - qi + half, lo, hi), 0, 0)),
        ],
        out_specs=pl.BlockSpec((1, t, DIFF_VD), lambda bi, h, qi, ki: (bi, qi, h)),
        out_shape=jax.ShapeDtypeStruct((b, s, n), BF16),
        scratch_shapes=[
            pltpu.VMEM((1, 2 * t), F32), pltpu.VMEM((1, 2 * t), F32),
            pltpu.VMEM((DIFF_VD, 2 * t), F32),
        ],
        compiler_params=_params("parallel", "parallel", "parallel", "arbitrary"),
        name="diff_attn",
    )(lam, subln_col, q, k, vt, bias)


def _out_proj_kernel(x_ref, o_ref, w_ref, g_ref, y_ref):
    y_ref[...] = x_ref[...] + _rms(_dot(o_ref[...], w_ref[...]), g_ref[...])


def _out_proj(x2, o2, w, g):
    t, d = x2.shape
    tm = min(ROW_TILE, t)
    return pl.pallas_call(
        _out_proj_kernel,
        grid=(t // tm,),
        in_specs=[
            pl.BlockSpec((tm, d), lambda i: (i, 0)),
            pl.BlockSpec((tm, o2.shape[1]), lambda i: (i, 0)),
            _const_spec(w.shape), _const_spec(g.shape),
        ],
        out_specs=pl.BlockSpec((tm, d), lambda i: (i, 0)),
        out_shape=jax.ShapeDtypeStruct((t, d), F32),
        compiler_params=_params("parallel"),
        name="out_proj",
    )(x2, o2, w, g)


def _mem_kv_kernel(m_ref, g_ref, w_ref, kv_ref):
    kv_ref[0] = _dot(_rms(m_ref[0], g_ref[...]).astype(BF16), w_ref[...]).astype(BF16)


def _mem_kv(mem, g, w):
    b, n_mem, d = mem.shape
    return pl.pallas_call(
        _mem_kv_kernel,
        grid=(b,),
        in_specs=[pl.BlockSpec((1, n_mem, d), lambda i: (i, 0, 0)),
                  _const_spec(g.shape), _const_spec(w.shape)],
        out_specs=pl.BlockSpec((1, n_mem, w.shape[1]), lambda i: (i, 0, 0)),
        out_shape=jax.ShapeDtypeStruct((b, n_mem, w.shape[1]), BF16),
        compiler_params=_params("parallel"),
        name="mem_kv",
    )(mem, g, w)


def _xattn_kernel(x_ref, kv_ref, g_in_ref, w_q_ref, w_o_ref, g_out_ref, y_ref):
    x = x_ref[0]
    d = x.shape[1]
    hd = d // XATTN_HEADS
    q = (_dot(_rms(x, g_in_ref[...]).astype(BF16), w_q_ref[...]) * (hd ** -0.5)).astype(BF16)
    heads = []
    for h in range(XATTN_HEADS):
        k = kv_ref[0, :, h * hd:(h + 1) * hd]
        v = kv_ref[0, :, d + h * hd:d + (h + 1) * hd]
        s = _dot_nt(q[:, h * hd:(h + 1) * hd], k)
        p = jnp.exp(s - jnp.max(s, axis=-1, keepdims=True))
        o = _dot(p.astype(BF16), v) / jnp.sum(p, axis=-1, keepdims=True)
        heads.append(o.astype(BF16))
    o = jnp.concatenate(heads, axis=1)
    y_ref[0] = x + _rms(_dot(o, w_o_ref[...]), g_out_ref[...])


def _xattn(x, kv, g_in, w_q, w_o, g_out):
    b, s, d = x.shape
    tm = min(ROW_TILE, s)
    return pl.pallas_call(
        _xattn_kernel,
        grid=(b, s // tm),
        in_specs=[
            pl.BlockSpec((1, tm, d), lambda i, j: (i, j, 0)),
            pl.BlockSpec((1,) + kv.shape[1:], lambda i, j: (i, 0, 0)),
            _const_spec(g_in.shape), _const_spec(w_q.shape), _const_spec(w_o.shape),
            _const_spec(g_out.shape),
        ],
        out_specs=pl.BlockSpec((1, tm, d), lambda i, j: (i, j, 0)),
        out_shape=jax.ShapeDtypeStruct((b, s, d), F32),
        compiler_params=_params("parallel", "parallel"),
        name="xattn",
    )(x, kv, g_in, w_q, w_o, g_out)


def _mlp_kernel(x_ref, g_in_ref, w1_ref, w2_ref, g_out_ref, y_ref, *, ff_chunk):
    x = x_ref[...]
    h = _rms(x, g_in_ref[...]).astype(BF16)
    d_ff = w1_ref.shape[1]
    acc = jnp.zeros(x.shape, F32)
    for c in range(0, d_ff, ff_chunk):
        a = jnp.maximum(_dot(h, w1_ref[:, c:c + ff_chunk]), 0.0)
        acc = acc + _dot((a * a).astype(BF16), w2_ref[c:c + ff_chunk, :])
    y_ref[...] = x + _rms(acc, g_out_ref[...])


def _mlp(x2, g_in, w1, w2, g_out):
    t, d = x2.shape
    tm = min(ROW_TILE, t)
    return pl.pallas_call(
        functools.partial(_mlp_kernel, ff_chunk=1024),
        grid=(t // tm,),
        in_specs=[
            pl.BlockSpec((tm, d), lambda i: (i, 0)),
            _const_spec(g_in.shape), _const_spec(w1.shape), _const_spec(w2.shape),
            _const_spec(g_out.shape),
        ],
        out_specs=pl.BlockSpec((tm, d), lambda i: (i, 0)),
        out_shape=jax.ShapeDtypeStruct((t, d), F32),
        compiler_params=_params("parallel"),
        name="mlp",
    )(x2, g_in, w1, w2, g_out)


def _t5_bucket(rel):
    nb = N_BUCKETS // 2
    max_exact = nb // 2
    ret = (rel > 0).astype(jnp.int32) * nb
    n = jnp.abs(rel)
    large = max_exact + (jnp.log(jnp.maximum(n, 1).astype(F32) / max_exact)
                         / math.log(MAX_DISTANCE / max_exact) * (nb - max_exact)).astype(jnp.int32)
    large = jnp.minimum(large, nb - 1)
    return ret + jnp.where(n < max_exact, n, large)


def _bucket_tiles(t):
    assert t >= MAX_DISTANCE
    off = (jnp.arange(N_BIAS_TILES, dtype=jnp.int32) - N_BIAS_TILES // 2)[:, None, None] * t
    kk = jnp.arange(t, dtype=jnp.int32)[None, :, None]
    qq = jnp.arange(t, dtype=jnp.int32)[None, None, :]
    return _t5_bucket(off + kk - qq)


def _rope_tables(s, q_scale):
    half = MLA_ROPE // 2
    freqs = ROPE_THETA ** (-jnp.arange(half, dtype=F32) / half)
    ang = jnp.arange(s, dtype=jnp.int32).astype(F32)[:, None] * freqs[None, :]
    cos, sin = jnp.cos(ang), jnp.sin(ang)
    tab = jnp.concatenate([cos, cos, -sin, sin], axis=1)
    return tab * q_scale, tab


def _swap_halves(w):
    half = w.shape[-1] // 2
    return jnp.concatenate([w[..., half:], w[..., :half]], axis=-1)


def _prep_mla(w_in, w_uq, w_ukv):
    lat = Q_LORA + KV_LORA
    w_kr = w_in[:, lat:]
    w_in_aug = jnp.concatenate([w_in[:, :lat], w_kr, _swap_halves(w_kr)], axis=1).astype(BF16)
    wq = w_uq.reshape(Q_LORA, MLA_HEADS, MLA_NOPE + MLA_ROPE)
    wq_r = wq[..., MLA_NOPE:]
    w_uq_aug = jnp.concatenate([wq[..., :MLA_NOPE], wq_r, _swap_halves(wq_r)], axis=-1)
    w_uq_aug = w_uq_aug.reshape(Q_LORA, MLA_HEADS * MLA_QK_PAD).astype(BF16)
    wkv = w_ukv.reshape(KV_LORA, MLA_HEADS, MLA_NOPE + MLA_V)
    w_k = wkv[..., :MLA_NOPE].reshape(KV_LORA, MLA_HEADS * MLA_NOPE).astype(BF16)
    w_vt = wkv[..., MLA_NOPE:].reshape(KV_LORA, MLA_HEADS * MLA_V).T.astype(BF16)
    return w_in_aug, w_uq_aug, w_k, w_vt


def _row(v):
    return v.reshape(1, -1).astype(F32)


def _trunk(x, mem, norm_gains, bias_tiles, mla_p, diff_p, xattn_p, mlp_p):
    b, s, d = x.shape
    depth = norm_gains.shape[0]
    for i in range(depth):
        g = norm_gains[i]
        j = i // 2
        if i % 2 == 0:
            w_in_aug, w_uq_aug, w_k, w_vt, qg, kvg, w_o = mla_p[j]
            q_scale = (MLA_NOPE + MLA_ROPE) ** -0.5 * LOG2E
            tabq, tabk = _rope_tables(s, q_scale)
            q, kn, kr, vt = _mla_proj(x, _row(g[0]), w_in_aug, qg, kvg, w_uq_aug, w_k, w_vt,
                                      tabq, tabk, q_scale)
            o = _mla_attn(q, kn, kr, vt)
        else:
            w_q, w_k, w_vt, lam, subln_col, w_o = diff_p[j]
            lambda_init = 0.8 - 0.6 * math.exp(-0.3 * i)
            q, k, vt = _diff_proj(x, _row(g[0]), w_q, w_k, w_vt, DIFF_HD ** -0.5 * LOG2E)
            o = _diff_attn(lam, subln_col, q, k, vt, bias_tiles, lambda_init)
        x2 = _out_proj(x.reshape(b * s, d), o.reshape(b * s, -1), w_o, _row(g[1]))
        mem_g, w_xq, w_xkv, w_xo = xattn_p[i]
        kv = _mem_kv(mem, mem_g, w_xkv)
        x = _xattn(x2.reshape(b, s, d), kv, _row(g[2]), w_xq, w_xo, _row(g[3]))
        w1, w2 = mlp_p[i]
        x = _mlp(x.reshape(b * s, d), _row(g[4]), w1, w2, _row(g[5])).reshape(b, s, d)
    return x


def kernel(x_prompt, x_sample, mem_prompt, mem_sample, norm_gains, rel_bias_table, mla_w_in, mla_q_norm, mla_kv_norm, mla_w_uq, mla_w_ukv, mla_w_o, diff_w_in, diff_lambda, diff_subln, diff_w_o, xattn_mem_norm, xattn_w_q, xattn_w_kv, xattn_w_o, mlp_w1, mlp_w2):
    depth = norm_gains.shape[0]
    mla_p = []
    for j in range(mla_w_in.shape[0]):
        mla_p.append(_prep_mla(mla_w_in[j], mla_w_uq[j], mla_w_ukv[j])
                     + (_row(mla_q_norm[j]), _row(mla_kv_norm[j]), mla_w_o[j].astype(BF16)))
    diff_p = []
    for j in range(diff_w_in.shape[0]):
        w_q, w_k, w_v = jnp.split(diff_w_in[j], 3, axis=1)
        diff_p.append((w_q.astype(BF16), w_k.astype(BF16), w_v.T.astype(BF16),
                       diff_lambda[j].astype(F32), diff_subln[j].reshape(-1, 1).astype(F32),
                       diff_w_o[j].astype(BF16)))
    xattn_p = [(_row(xattn_mem_norm[i]), xattn_w_q[i].astype(BF16), xattn_w_kv[i].astype(BF16),
                xattn_w_o[i].astype(BF16)) for i in range(depth)]
    mlp_p = [(mlp_w1[i].astype(BF16), mlp_w2[i].astype(BF16)) for i in range(depth)]

    outs = []
    for x, mem in ((x_prompt, mem_prompt), (x_sample, mem_sample)):
        t = min(ATTN_TILE, x.shape[1])
        bias_tiles = _bias_tiles(rel_bias_table.astype(F32), _bucket_tiles(t))
        outs.append(_trunk(x, mem, norm_gains, bias_tiles, mla_p, diff_p, xattn_p, mlp_p))
    return tuple(outs)
```
